```python
import jax, jax.numpy as jnp
from jax import lax
import numpy as np

D_MODEL = 1024
BATCH = 16
SEQ = 2048
DEPTH = 1

D_MIX = 2 * D_MODEL
D_SSD = D_MIX // 2
SSD_HEAD_DIM = 64
SSD_HEADS = D_SSD // SSD_HEAD_DIM
SSD_GROUPS = 2
HEADS_PER_GROUP = SSD_HEADS // SSD_GROUPS
D_STATE = 128
SSD_CONV = 4
CHUNK = 128
D_XBC = D_SSD + 2 * SSD_GROUPS * D_STATE
D_CF = D_MIX - D_SSD
CF_KERNEL = 31
D_FF = 4 * D_MODEL
D_IN_PROJ = D_SSD + D_XBC + SSD_HEADS + 2 * D_CF
EPS = 1e-5

kernel_name = "hymba_style_ssd_conformer_hybrid"


def rmsnorm(x, w):
    xf = x.astype(jnp.float32)
    y = xf * lax.rsqrt(jnp.mean(xf * xf, axis=-1, keepdims=True) + EPS)
    return (y * w.astype(jnp.float32)).astype(x.dtype)


def layernorm(x, w, b):
    xf = x.astype(jnp.float32)
    mu = jnp.mean(xf, axis=-1, keepdims=True)
    var = jnp.mean(jnp.square(xf - mu), axis=-1, keepdims=True)
    y = (xf - mu) * lax.rsqrt(var + EPS)
    return (y * w.astype(jnp.float32) + b.astype(jnp.float32)).astype(x.dtype)


def gated_rmsnorm(y, z, w):
    g = (y * jax.nn.silu(z)).astype(jnp.float32)
    shp = g.shape
    g = g.reshape(shp[:-1] + (SSD_GROUPS, shp[-1] // SSD_GROUPS))
    g = g * lax.rsqrt(jnp.mean(g * g, axis=-1, keepdims=True) + EPS)
    return (g.reshape(shp) * w.astype(jnp.float32)).astype(y.dtype)


def causal_dwconv(u, w, b):
    k = w.shape[0]
    out = lax.conv_general_dilated(
        u, w[:, None, :].astype(u.dtype), window_strides=(1,), padding=[(k - 1, 0)],
        dimension_numbers=("NWC", "WIO", "NWC"), feature_group_count=u.shape[-1])
    return out + b.astype(u.dtype)


def ssd_chunked(xh, dt, a, bm, cm):
    b, s, _, p = xh.shape
    nc = s // CHUNK
    xr = (xh.astype(jnp.float32) * dt[..., None]).reshape(b, nc, CHUNK, SSD_GROUPS, HEADS_PER_GROUP, p)
    la = (dt * a).reshape(b, nc, CHUNK, SSD_GROUPS, HEADS_PER_GROUP)
    la = jnp.moveaxis(la, 2, -1)
    a_cs = jnp.cumsum(la, axis=-1)
    br = bm.astype(jnp.float32).reshape(b, nc, CHUNK, SSD_GROUPS, D_STATE)
    cr = cm.astype(jnp.float32).reshape(b, nc, CHUNK, SSD_GROUPS, D_STATE)
    causal = jnp.tril(jnp.ones((CHUNK, CHUNK), dtype=bool))
    seg = a_cs[..., :, None] - a_cs[..., None, :]
    decay = jnp.exp(jnp.where(causal, seg, -jnp.inf))
    cb = jnp.einsum("bclgn,bcsgn->bcgls", cr, br)
    y_diag = jnp.einsum("bcgls,bcghls,bcsghp->bclghp", cb, decay, xr)
    decay_st = jnp.exp(a_cs[..., -1:] - a_cs)
    states = jnp.einsum("bclgn,bcghl,bclghp->bcghpn", br, decay_st, xr)
    chunk_decay = jnp.exp(a_cs[..., -1])

    def step(hstate, inp):
        st, dc = inp
        return hstate * dc[..., None, None] + st, hstate

    h0 = jnp.zeros((b, SSD_GROUPS, HEADS_PER_GROUP, p, D_STATE), jnp.float32)
    _, prev = lax.scan(step, h0, (jnp.moveaxis(states, 1, 0), jnp.moveaxis(chunk_decay, 1, 0)))
    prev = jnp.moveaxis(prev, 0, 1)
    y_off = jnp.einsum("bclgn,bcghpn,bcghl->bclghp", cr, prev, jnp.exp(a_cs))
    return (y_diag + y_off).reshape(b, s, SSD_HEADS, p)


def hybrid_mixer(h, w_in, conv_ssd_w, conv_ssd_b, dt_bias, a_log, d_skip, ssd_norm_w,
                 conv_cf_w, conv_cf_b, cf_ln_w, cf_ln_b, w_out):
    b, s, _ = h.shape
    proj = h @ w_in.astype(h.dtype)
    i1 = D_SSD
    i2 = i1 + D_XBC
    i3 = i2 + SSD_HEADS
    i4 = i3 + D_CF
    z, xbc, dt_raw, glu_a, glu_b = jnp.split(proj, [i1, i2, i3, i4], axis=-1)
    xbc = jax.nn.silu(causal_dwconv(xbc, conv_ssd_w, conv_ssd_b))
    xs, bm, cm = jnp.split(xbc, [D_SSD, D_SSD + SSD_GROUPS * D_STATE], axis=-1)
    dt = jax.nn.softplus(dt_raw.astype(jnp.float32) + dt_bias.astype(jnp.float32))
    a = -jnp.exp(a_log.astype(jnp.float32))
    xh = xs.reshape(b, s, SSD_HEADS, SSD_HEAD_DIM)
    bm = bm.reshape(b, s, SSD_GROUPS, D_STATE)
    cm = cm.reshape(b, s, SSD_GROUPS, D_STATE)
    y = ssd_chunked(xh, dt, a, bm, cm) + d_skip.astype(jnp.float32)[:, None] * xh.astype(jnp.float32)
    y = gated_rmsnorm(y.reshape(b, s, D_SSD).astype(h.dtype), z, ssd_norm_w)
    u = glu_a * jax.nn.sigmoid(glu_b)
    u = causal_dwconv(u, conv_cf_w, conv_cf_b)
    u = jax.nn.silu(layernorm(u, cf_ln_w, cf_ln_b))
    return jnp.concatenate([y, u], axis=-1) @ w_out.astype(h.dtype)


def setup_inputs(seed: int = 0) -> dict:
    key = jax.random.key(seed)
    ks = jax.random.split(key, 24)
    nrm = lambda k, shp, sc: jax.random.normal(k, shp, jnp.float32) * sc
    L = DEPTH
    x = jax.random.normal(ks[0], (BATCH, SEQ, D_MODEL), jnp.float32)
    c = jax.random.normal(ks[1], (BATCH, D_MODEL), jnp.float32)
    w_ada = nrm(ks[2], (L, D_MODEL, 6 * D_MODEL), D_MODEL ** -0.5)
    b_ada = nrm(ks[3], (L, 6 * D_MODEL), 0.02)
    norm_mix_w = 1.0 + nrm(ks[4], (L, D_MODEL), 0.02)
    w_in = nrm(ks[5], (L, D_MODEL, D_IN_PROJ), D_MODEL ** -0.5)
    conv_ssd_w = nrm(ks[6], (L, SSD_CONV, D_XBC), SSD_CONV ** -0.5)
    conv_ssd_b = nrm(ks[7], (L, D_XBC), 0.02)
    dt0 = jnp.exp(jax.random.uniform(ks[8], (L, SSD_HEADS), jnp.float32,
                                     np.log(1e-3).astype(np.float32), np.log(1e-1).astype(np.float32)))
    dt_bias = dt0 + jnp.log(-jnp.expm1(-dt0))
    a_log = jnp.log(jax.random.uniform(ks[9], (L, SSD_HEADS), jnp.float32, 1.0, 16.0))
    d_skip = 1.0 + nrm(ks[10], (L, SSD_HEADS), 0.1)
    ssd_norm_w = 1.0 + nrm(ks[11], (L, D_SSD), 0.02)
    conv_cf_w = nrm(ks[12], (L, CF_KERNEL, D_CF), CF_KERNEL ** -0.5)
    conv_cf_b = nrm(ks[13], (L, D_CF), 0.02)
    cf_ln_w = 1.0 + nrm(ks[14], (L, D_CF), 0.02)
    cf_ln_b = nrm(ks[15], (L, D_CF), 0.02)
    w_out = nrm(ks[16], (L, D_MIX, D_MODEL), D_MIX ** -0.5)
    norm_mlp_w = 1.0 + nrm(ks[17], (L, D_MODEL), 0.02)
    w_mlp1 = nrm(ks[18], (L, D_MODEL, D_FF), D_MODEL ** -0.5)
    w_mlp2 = nrm(ks[19], (L, D_FF, D_MODEL), D_FF ** -0.5)
    norm_final_w = 1.0 + nrm(ks[20], (D_MODEL,), 0.02)
    return {"x": x, "c": c, "w_ada": w_ada, "b_ada": b_ada, "norm_mix_w": norm_mix_w,
            "w_in": w_in, "conv_ssd_w": conv_ssd_w, "conv_ssd_b": conv_ssd_b,
            "dt_bias": dt_bias, "a_log": a_log, "d_skip": d_skip, "ssd_norm_w": ssd_norm_w,
            "conv_cf_w": conv_cf_w, "conv_cf_b": conv_cf_b, "cf_ln_w": cf_ln_w, "cf_ln_b": cf_ln_b,
            "w_out": w_out, "norm_mlp_w": norm_mlp_w, "w_mlp1": w_mlp1, "w_mlp2": w_mlp2,
            "norm_final_w": norm_final_w}


def reference(x, c, w_ada, b_ada, norm_mix_w, w_in, conv_ssd_w, conv_ssd_b, dt_bias, a_log,
              d_skip, ssd_norm_w, conv_cf_w, conv_cf_b, cf_ln_w, cf_ln_b, w_out,
              norm_mlp_w, w_mlp1, w_mlp2, norm_final_w):
    c_act = jax.nn.silu(c)
    for l in range(DEPTH):
        mod = (c_act @ w_ada[l].astype(c.dtype) + b_ada[l].astype(c.dtype))[:, None, :]
        sh_mix, sc_mix, g_mix, sh_mlp, sc_mlp, g_mlp = jnp.split(mod, 6, axis=-1)
        h = rmsnorm(x, norm_mix_w[l]) * (1.0 + sc_mix) + sh_mix
        x = x + g_mix * hybrid_mixer(h, w_in[l], conv_ssd_w[l], conv_ssd_b[l], dt_bias[l], a_log[l],
                                     d_skip[l], ssd_norm_w[l], conv_cf_w[l], conv_cf_b[l],
                                     cf_ln_w[l], cf_ln_b[l], w_out[l])
        h = rmsnorm(x, norm_mlp_w[l]) * (1.0 + sc_mlp) + sh_mlp
        x = x + g_mlp * (jnp.square(jax.nn.relu(h @ w_mlp1[l].astype(h.dtype))) @ w_mlp2[l].astype(h.dtype))
    return rmsnorm(x, norm_final_w)
```

```python
import functools

import jax
import jax.numpy as jnp
from jax import lax
from jax.experimental import pallas as pl
from jax.experimental.pallas import tpu as pltpu

D_MODEL = 1024
D_SSD = 1024
SSD_HEAD_DIM = 64
SSD_HEADS = 16
SSD_GROUPS = 2
HEADS_PER_GROUP = 8
D_STATE = 128
SSD_CONV = 4
CHUNK = 128
D_BC = 2 * SSD_GROUPS * D_STATE
D_XBC = D_SSD + D_BC
D_CF = 1024
CF_KERNEL = 31
D_FF = 4096
EPS = 1e-5

LANES = 128
DT_PAD = LANES
C_Z = 0
C_XBC = C_Z + D_SSD
C_GA = C_XBC + D_XBC
C_GB = C_GA + D_CF
C_DT = C_GB + D_CF
D_IN_R = C_DT + DT_PAD

SSD_HALO = 8
CF_HALO = 32
VMEM_LIMIT = 56 * 1024 * 1024

F32 = jnp.float32
BF16 = jnp.bfloat16


def _silu(v):
    return v * jax.nn.sigmoid(v)


def _softplus(v):
    return jnp.maximum(v, 0.0) + jnp.log1p(jnp.exp(-jnp.abs(v)))


def _bdot(a, b):
    return jnp.dot(a, b, preferred_element_type=F32)


def _ada_kernel(c_ref, w_ref, b_ref, o_ref):
    ca = _silu(c_ref[...])
    o_ref[...] = _bdot(ca.astype(BF16), w_ref[...].astype(BF16)) + b_ref[...]


def _ada(c, w_ada, b_ada):
    b, d = c.shape
    n = w_ada.shape[1]
    tn = 1024
    return pl.pallas_call(
        _ada_kernel,
        out_shape=jax.ShapeDtypeStruct((b, n), F32),
        grid=(n // tn,),
        in_specs=[pl.BlockSpec((b, d), lambda j: (0, 0)),
                  pl.BlockSpec((d, tn), lambda j: (0, j)),
                  pl.BlockSpec((1, tn), lambda j: (0, j))],
        out_specs=pl.BlockSpec((b, tn), lambda j: (0, j)),
        compiler_params=pltpu.CompilerParams(dimension_semantics=("arbitrary",)),
        name="ada_mod",
    )(c, w_ada, b_ada.reshape(1, n))


def _inproj_kernel(x_ref, sh_ref, sc_ref, nw_ref, w_ref, cw_ref, cb_ref, dtb_ref,
                   fw_ref, fb_ref, lnw_ref, lnb_ref,
                   z_ref, xs_ref, bc_ref, dt_ref, u_ref,
                   xbc_buf, u_buf, uc_buf, *, tm):
    s = pl.program_id(1)

    @pl.when(s == 0)
    def _():
        xbc_buf[0:SSD_HALO, :] = jnp.zeros((SSD_HALO, D_XBC), F32)
        u_buf[0:CF_HALO, :] = jnp.zeros((CF_HALO, D_CF), F32)

    x = x_ref[...]
    ms = jnp.mean(x * x, axis=-1, keepdims=True)
    h = (x * lax.rsqrt(ms + EPS) * nw_ref[...]) * (1.0 + sc_ref[...]) + sh_ref[...]
    hb = h.astype(BF16)

    z_ref[...] = _bdot(hb, w_ref[:, C_Z:C_Z + D_SSD]).astype(BF16)
    dt_ref[...] = _softplus(_bdot(hb, w_ref[:, C_DT:C_DT + DT_PAD]) + dtb_ref[...])

    xbc_buf[SSD_HALO:SSD_HALO + tm, :] = _bdot(hb, w_ref[:, C_XBC:C_XBC + D_XBC])
    for lb in range(D_XBC // 512):
        cols = slice(lb * 512, (lb + 1) * 512)
        acc = jnp.broadcast_to(cb_ref[:, cols], (tm, 512))
        for j in range(SSD_CONV):
            off = SSD_HALO - (SSD_CONV - 1) + j
            acc = acc + cw_ref[j:j + 1, cols] * xbc_buf[off:off + tm, cols]
        act = _silu(acc).astype(BF16)
        if lb < D_SSD // 512:
            xs_ref[:, cols] = act
        else:
            bc_ref[:, lb * 512 - D_SSD:(lb + 1) * 512 - D_SSD] = act
    xbc_buf[0:SSD_HALO, :] = xbc_buf[tm:tm + SSD_HALO, :]

    ga = _bdot(hb, w_ref[:, C_GA:C_GA + D_CF])
    gb = _bdot(hb, w_ref[:, C_GB:C_GB + D_CF])
    u_buf[CF_HALO:CF_HALO + tm, :] = ga * jax.nn.sigmoid(gb)
    rb = 32
    for r in range(tm // rb):
        for lb in range(D_CF // 512):
            cols = slice(lb * 512, (lb + 1) * 512)
            acc = jnp.broadcast_to(fb_ref[:, cols], (rb, 512))
            for j in range(CF_KERNEL):
                off = CF_HALO - (CF_KERNEL - 1) + j + r * rb
                acc = acc + fw_ref[j:j + 1, cols] * u_buf[off:off + rb, cols]
            uc_buf[r * rb:(r + 1) * rb, cols] = acc
    u_buf[0:CF_HALO, :] = u_buf[tm:tm + CF_HALO, :]

    uc = uc_buf[...]
    mu = jnp.mean(uc, axis=-1, keepdims=True)
    d = uc - mu
    var = jnp.mean(d * d, axis=-1, keepdims=True)
    y = d * lax.rsqrt(var + EPS) * lnw_ref[...] + lnb_ref[...]
    u_ref[...] = _silu(y).astype(BF16)


def _inproj(x, sh, sc, nw, w_r, cw, cb, dtb, fw, fb, lnw, lnb, *, tm):
    b, s, d = x.shape
    const = lambda shape: pl.BlockSpec(shape, lambda i, j: (0,) * len(shape),
                                       pipeline_mode=pl.Buffered(1))
    perb = pl.BlockSpec((None, 1, d), lambda i, j: (i, 0, 0))
    tok = lambda n: pl.BlockSpec((None, tm, n), lambda i, j: (i, j, 0))
    return pl.pallas_call(
        functools.partial(_inproj_kernel, tm=tm),
        out_shape=(jax.ShapeDtypeStruct((b, s, D_SSD), BF16),
                   jax.ShapeDtypeStruct((b, s, D_SSD), BF16),
                   jax.ShapeDtypeStruct((b, s, D_BC), BF16),
                   jax.ShapeDtypeStruct((b, s, DT_PAD), F32),
                   jax.ShapeDtypeStruct((b, s, D_CF), BF16)),
        grid=(b, s // tm),
        in_specs=[tok(d), perb, perb, const((1, d)), const((d, D_IN_R)),
                  const((SSD_CONV, D_XBC)), const((1, D_XBC)), const((1, DT_PAD)),
                  const((CF_HALO, D_CF)), const((1, D_CF)), const((1, D_CF)), const((1, D_CF))],
        out_specs=(tok(D_SSD), tok(D_SSD), tok(D_BC), tok(DT_PAD), tok(D_CF)),
        scratch_shapes=[pltpu.VMEM((SSD_HALO + tm, D_XBC), F32),
                        pltpu.VMEM((CF_HALO + tm, D_CF), F32),
                        pltpu.VMEM((tm, D_CF), F32)],
        compiler_params=pltpu.CompilerParams(dimension_semantics=("arbitrary", "arbitrary"),
                                             vmem_limit_bytes=VMEM_LIMIT),
        name="inproj_conv",
    )(x, sh, sc, nw, w_r, cw, cb, dtb, fw, fb, lnw, lnb)


def _ssd_kernel(xs_ref, bc_ref, dt_ref, z_ref, alog_ref, dskip_ref, nw_ref, y_ref,
                state_ref, y_buf):
    @pl.when(pl.program_id(1) == 0)
    def _():
        state_ref[...] = jnp.zeros_like(state_ref)

    n = CHUNK
    dt = dt_ref[...]
    la = dt * (-jnp.exp(alog_ref[...]))
    ri = lax.broadcasted_iota(jnp.int32, (n, n), 0)
    ci = lax.broadcasted_iota(jnp.int32, (n, n), 1)
    causal = ri >= ci
    a_cs = jnp.dot(causal.astype(F32), la, precision=lax.Precision.HIGHEST,
                   preferred_element_type=F32)
    a_cs_t = a_cs.T
    dt_t = dt.T
    w_t = dt_t * jnp.exp(a_cs_t[:, n - 1:n] - a_cs_t)
    lo = ci < SSD_HEAD_DIM

    for g in range(SSD_GROUPS):
        bg = bc_ref[:, g * D_STATE:(g + 1) * D_STATE]
        cg = bc_ref[:, (SSD_GROUPS + g) * D_STATE:(SSD_GROUPS + g + 1) * D_STATE]
        cb = lax.dot_general(cg, bg, (((1,), (1,)), ((), ())), preferred_element_type=F32)
        bg_t = bg.astype(F32).T
        cg_f = cg.astype(F32)
        for pr in range(HEADS_PER_GROUP // 2):
            h0 = g * HEADS_PER_GROUP + 2 * pr
            cols = slice(h0 * SSD_HEAD_DIM, (h0 + 2) * SSD_HEAD_DIM)
            scols = slice(pr * LANES, (pr + 1) * LANES)
            xp = xs_ref[:, cols]
            zero = jnp.zeros_like(xp)
            x_lo = jnp.where(lo, xp, zero)
            x_hi = jnp.where(lo, zero, xp)
            st = state_ref[g, :, scols]
            stb = st.astype(BF16)
            s_lo = jnp.where(lo, stb, zero)
            s_hi = jnp.where(lo, zero, stb)
            m_parts, c_parts, b_parts, decs = [], [], [], []
            for h in (h0, h0 + 1):
                a_col = jnp.broadcast_to(a_cs[:, h:h + 1], (n, n))
                seg = a_col - a_cs_t[h:h + 1, :]
                decay = jnp.exp(jnp.where(causal, seg, -jnp.inf))
                m_parts.append((cb * decay * dt_t[h:h + 1, :]).astype(BF16))
                c_parts.append((cg_f * jnp.exp(a_col)).astype(BF16))
                b_parts.append((bg_t * w_t[h:h + 1, :]).astype(BF16))
                decs.append(jnp.broadcast_to(jnp.exp(a_cs[n - 1:n, h:h + 1]), (1, LANES)))
            lhs = jnp.concatenate(m_parts + c_parts, axis=1)
            rhs = jnp.concatenate([x_lo, x_hi, s_lo, s_hi], axis=0)
            y_buf[:, cols] = _bdot(lhs, rhs)
            new = _bdot(jnp.concatenate(b_parts, axis=1), jnp.concatenate([x_lo, x_hi], axis=0))
            dec = jnp.where(lo[0:1, :], decs[0], decs[1])
            state_ref[g, :, scols] = st * dec + new

    y = y_buf[...] + dskip_ref[...] * xs_ref[...].astype(F32)
    gt = y * _silu(z_ref[...].astype(F32))
    gw = D_SSD // SSD_GROUPS
    for g in range(SSD_GROUPS):
        cols = slice(g * gw, (g + 1) * gw)
        gg = gt[:, cols]
        ms = jnp.mean(gg * gg, axis=-1, keepdims=True)
        y_ref[:, cols] = (gg * lax.rsqrt(ms + EPS) * nw_ref[:, cols]).astype(BF16)


def _ssd(xs, bc, dt, z, alog, dskip, nw):
    b, s, _ = xs.shape
    const = lambda shape: pl.BlockSpec(shape, lambda i, j: (0,) * len(shape))
    tok = lambda n: pl.BlockSpec((None, CHUNK, n), lambda i, j: (i, j, 0))
    return pl.pallas_call(
        _ssd_kernel,
        out_shape=jax.ShapeDtypeStruct((b, s, D_SSD), BF16),
        grid=(b, s // CHUNK),
        in_specs=[tok(D_SSD), tok(D_BC), tok(DT_PAD), tok(D_SSD),
                  const((1, DT_PAD)), const((1, D_SSD)), const((1, D_SSD))],
        out_specs=tok(D_SSD),
        scratch_shapes=[pltpu.VMEM((SSD_GROUPS, D_STATE, HEADS_PER_GROUP * SSD_HEAD_DIM), F32),
                        pltpu.VMEM((CHUNK, D_SSD), F32)],
        compiler_params=pltpu.CompilerParams(dimension_semantics=("arbitrary", "arbitrary"),
                                             vmem_limit_bytes=VMEM_LIMIT),
        name="ssd_scan",
    )(xs, bc, dt, z, alog, dskip, nw)


def _tail_kernel(x_ref, y_ref, u_ref, gmix_ref, shm_ref, scm_ref, gmlp_ref, nw_ref, fw_ref,
                 wo_ref, w1_ref, w2_ref, o_ref, acc_ref, *, ff_chunk):
    mix = _bdot(y_ref[...], wo_ref[0:D_SSD, :]) + _bdot(u_ref[...], wo_ref[D_SSD:D_SSD + D_CF, :])
    x1 = x_ref[...] + gmix_ref[...] * mix
    ms = jnp.mean(x1 * x1, axis=-1, keepdims=True)
    h = (x1 * lax.rsqrt(ms + EPS) * nw_ref[...]) * (1.0 + scm_ref[...]) + shm_ref[...]
    hb = h.astype(BF16)
    for k in range(D_FF // ff_chunk):
        cols = slice(k * ff_chunk, (k + 1) * ff_chunk)
        hid = jnp.maximum(_bdot(hb, w1_ref[:, cols]), 0.0)
        part = _bdot((hid * hid).astype(BF16), w2_ref[cols, :])
        if k == 0:
            acc_ref[...] = part
        else:
            acc_ref[...] += part
    x2 = x1 + gmlp_ref[...] * acc_ref[...]
    ms2 = jnp.mean(x2 * x2, axis=-1, keepdims=True)
    o_ref[...] = x2 * lax.rsqrt(ms2 + EPS) * fw_ref[...]


def _tail(x, y, u, gmix, shm, scm, gmlp, nw, fw, wo, w1, w2, *, tm, ff_chunk):
    b, s, d = x.shape
    const = lambda shape: pl.BlockSpec(shape, lambda i, j: (0,) * len(shape),
                                       pipeline_mode=pl.Buffered(1))
    perb = pl.BlockSpec((None, 1, d), lambda i, j: (i, 0, 0))
    tok = lambda n: pl.BlockSpec((None, tm, n), lambda i, j: (i, j, 0))
    return pl.pallas_call(
        functools.partial(_tail_kernel, ff_chunk=ff_chunk),
        out_shape=jax.ShapeDtypeStruct((b, s, d), F32),
        grid=(b, s // tm),
        in_specs=[tok(d), tok(D_SSD), tok(D_CF), perb, perb, perb, perb,
                  const((1, d)), const((1, d)),
                  const((D_SSD + D_CF, d)), const((d, D_FF)), const((D_FF, d))],
        out_specs=tok(d),
        scratch_shapes=[pltpu.VMEM((tm, d), F32)],
        compiler_params=pltpu.CompilerParams(dimension_semantics=("arbitrary", "arbitrary"),
                                             vmem_limit_bytes=VMEM_LIMIT),
        name="outproj_mlp",
    )(x, y, u, gmix, shm, scm, gmlp, nw, fw, wo, w1, w2)


def kernel(x, c, w_ada, b_ada, norm_mix_w, w_in, conv_ssd_w, conv_ssd_b, dt_bias, a_log, d_skip,
           ssd_norm_w, conv_cf_w, conv_cf_b, cf_ln_w, cf_ln_b, w_out, norm_mlp_w, w_mlp1, w_mlp2,
           norm_final_w):
    b, s, d = x.shape
    assert w_ada.shape[0] == 1, "single-layer stack only"
    l = 0
    i1 = D_SSD
    i2 = i1 + D_XBC
    i3 = i2 + SSD_HEADS
    i4 = i3 + D_CF
    row = lambda v: v.reshape(1, -1)
    pad_lanes = lambda v, n: jnp.pad(v, ((0, 0), (0, n - v.shape[1])))
    mod = _ada(c, w_ada[l], b_ada[l]).reshape(b, 6, 1, d)
    sh_mix, sc_mix, g_mix, sh_mlp, sc_mlp, g_mlp = (mod[:, k] for k in range(6))
    wi = w_in[l]
    w_r = jnp.concatenate([wi[:, :i1], wi[:, i1:i2], wi[:, i3:i4], wi[:, i4:],
                           pad_lanes(wi[:, i2:i3], DT_PAD)], axis=1).astype(BF16)
    fw = jnp.pad(conv_cf_w[l], ((0, CF_HALO - CF_KERNEL), (0, 0)))
    z, xs, bc, dt, u = _inproj(
        x, sh_mix, sc_mix, row(norm_mix_w[l]), w_r, conv_ssd_w[l], row(conv_ssd_b[l]),
        pad_lanes(row(dt_bias[l]), DT_PAD), fw, row(conv_cf_b[l]), row(cf_ln_w[l]),
        row(cf_ln_b[l]), tm=256)
    y = _ssd(xs, bc, dt, z, pad_lanes(row(a_log[l]), DT_PAD),
             row(jnp.repeat(d_skip[l], SSD_HEAD_DIM)), row(ssd_norm_w[l]))
    return _tail(x, y, u, g_mix, sh_mlp, sc_mlp, g_mlp, row(norm_mlp_w[l]), row(norm_final_w),
                 w_out[l].astype(BF16), w_mlp1[l].astype(BF16), w_mlp2[l].astype(BF16),
                 tm=256, ff_chunk=1024)
```
